```python
import jax
import jax.numpy as jnp
from jax import lax
import numpy as np

D_MODEL = 4096
BATCH = 8
SEQ = 2048
DEPTH = 2

CHUNK = 64
N_LEFT_CHUNKS = 8
BAND = N_LEFT_CHUNKS + 1
N_HEADS = 32
HEAD_DIM = D_MODEL // N_HEADS
H_A = 12
H_B = 10
H_C = 10
W_A = H_A * HEAD_DIM
W_B = H_B * HEAD_DIM
W_C = H_C * HEAD_DIM
MIX_WIDTH = W_A + W_B + W_C
PROJ_WIDTH = 3 * MIX_WIDTH + H_C
REL_CLIP = 128
Q_BLOCK = 128
PEER_HEADS = 8
PEER_QDIM = 256
PEER_HALF = PEER_QDIM // 2
N_KEYS = 128
N_EXPERTS = N_KEYS * N_KEYS
PEER_TOPK = 16
TOK_BLOCK = 128
RMS_EPS = 1e-6
NEG_INF = -1e30
FORGET_BIAS_INIT = 2.0

kernel_name = 'hybrid_streaming_relpos_stickbreak_fox_peer'


def rms_norm(x, g):
    xf = x.astype(jnp.float32)
    y = xf * lax.rsqrt(jnp.mean(xf * xf, axis=-1, keepdims=True) + RMS_EPS)
    return (y * g.astype(jnp.float32)).astype(x.dtype)


def _to_heads(t, n_heads):
    b, s, _ = t.shape
    return t.reshape(b, s, n_heads, HEAD_DIM).transpose(0, 2, 1, 3)


def _from_heads(t):
    b, h, s, dh = t.shape
    return t.transpose(0, 2, 1, 3).reshape(b, s, h * dh)


def _split_groups(t):
    ga = _to_heads(t[..., :W_A], H_A)
    gb = _to_heads(t[..., W_A:W_A + W_B], H_B)
    gc = _to_heads(t[..., W_A + W_B:], H_C)
    return ga, gb, gc


def chunked_relpos_attention(q, k, v, rel_bias):
    b, h, s, dh = q.shape
    nc = s // CHUNK
    qc = q.reshape(b, h, nc, CHUNK, dh)
    pad = ((0, 0), (0, 0), (N_LEFT_CHUNKS * CHUNK, 0), (0, 0))
    kp = jnp.pad(k, pad).reshape(b, h, nc + N_LEFT_CHUNKS, CHUNK, dh)
    vp = jnp.pad(v, pad).reshape(b, h, nc + N_LEFT_CHUNKS, CHUNK, dh)
    kb = jnp.stack([kp[:, :, j:j + nc] for j in range(BAND)], axis=3)
    vb = jnp.stack([vp[:, :, j:j + nc] for j in range(BAND)], axis=3)
    scores = jnp.einsum('bhcid,bhcjmd->bhcijm', qc, kb).astype(jnp.float32) * (dh ** -0.5)
    i_pos = jnp.arange(CHUNK)
    j_off = jnp.arange(BAND)
    rel = (N_LEFT_CHUNKS - j_off)[None, :, None] * CHUNK + i_pos[:, None, None] - i_pos[None, None, :]
    idx = jnp.clip(rel, -REL_CLIP, REL_CLIP) + REL_CLIP
    bias = rel_bias[:, idx].astype(jnp.float32)
    valid = (jnp.arange(nc)[:, None] - N_LEFT_CHUNKS + j_off[None, :]) >= 0
    logits = scores + bias[None, :, None]
    logits = jnp.where(valid[None, None, :, None, :, None], logits, NEG_INF)
    p = jax.nn.softmax(logits.reshape(b, h, nc, CHUNK, BAND * CHUNK), axis=-1)
    p = p.reshape(b, h, nc, CHUNK, BAND, CHUNK).astype(v.dtype)
    out = jnp.einsum('bhcijm,bhcjmd->bhcid', p, vb)
    return out.reshape(b, h, s, dh)


def stick_breaking_attention(q, k, v):
    b, h, s, dh = q.shape
    outs = []
    for blk in range(s // Q_BLOCK):
        q0 = blk * Q_BLOCK
        kend = q0 + Q_BLOCK
        z = jnp.einsum('bhtd,bhsd->bhts', q[:, :, q0:kend], k[:, :, :kend]).astype(jnp.float32) * (dh ** -0.5)
        t_pos = q0 + jnp.arange(Q_BLOCK)
        s_pos = jnp.arange(kend)
        strict = s_pos[None, :] < t_pos[:, None]
        log_1m = jnp.where(strict, jax.nn.log_sigmoid(-z), 0.0)
        tail = lax.cumsum(log_1m, axis=3, reverse=True) - log_1m
        a = jnp.where(strict, jnp.exp(jax.nn.log_sigmoid(z) + tail), 0.0)
        outs.append(jnp.einsum('bhts,bhsd->bhtd', a.astype(v.dtype), v[:, :, :kend]))
    return jnp.concatenate(outs, axis=2)


def forgetting_attention(q, k, v, log_f):
    b, h, s, dh = q.shape
    cum = lax.cumsum(log_f, axis=2)
    outs = []
    for blk in range(s // Q_BLOCK):
        q0 = blk * Q_BLOCK
        kend = q0 + Q_BLOCK
        z = jnp.einsum('bhtd,bhsd->bhts', q[:, :, q0:kend], k[:, :, :kend]).astype(jnp.float32) * (dh ** -0.5)
        z = z + cum[:, :, q0:kend, None] - cum[:, :, None, :kend]
        t_pos = q0 + jnp.arange(Q_BLOCK)
        s_pos = jnp.arange(kend)
        causal = s_pos[None, :] <= t_pos[:, None]
        p = jax.nn.softmax(jnp.where(causal, z, NEG_INF), axis=-1)
        outs.append(jnp.einsum('bhts,bhsd->bhtd', p.astype(v.dtype), v[:, :, :kend]))
    return jnp.concatenate(outs, axis=2)


def peer_ffn(h, w_pq, sub_k1, sub_k2, u_emb, v_emb):
    b, s, d = h.shape
    n_tok = b * s
    t = h.reshape(n_tok, d)
    q = jnp.einsum('td,de->te', t, w_pq).reshape(n_tok, PEER_HEADS, 2, PEER_HALF)
    s1 = jnp.einsum('thd,hnd->thn', q[:, :, 0], sub_k1).astype(jnp.float32)
    s2 = jnp.einsum('thd,hnd->thn', q[:, :, 1], sub_k2).astype(jnp.float32)
    v1, i1 = lax.top_k(s1, PEER_TOPK)
    v2, i2 = lax.top_k(s2, PEER_TOPK)
    cand = (v1[..., :, None] + v2[..., None, :]).reshape(n_tok, PEER_HEADS, PEER_TOPK * PEER_TOPK)
    cand_id = (i1[..., :, None] * N_KEYS + i2[..., None, :]).reshape(n_tok, PEER_HEADS, PEER_TOPK * PEER_TOPK)
    best, pos = lax.top_k(cand, PEER_TOPK)
    eid = jnp.take_along_axis(cand_id, pos, axis=-1)
    g = jax.nn.softmax(best, axis=-1)
    n_blk = n_tok // TOK_BLOCK

    def block(args):
        tb, eb, gb = args
        ub = jnp.take(u_emb, eb, axis=0)
        a = jnp.einsum('td,thkd->thk', tb, ub).astype(jnp.float32)
        w = gb * jax.nn.gelu(a, approximate=False)
        vb = jnp.take(v_emb, eb, axis=0)
        return jnp.einsum('thk,thkd->td', w.astype(tb.dtype), vb)

    out = lax.map(block, (t.reshape(n_blk, TOK_BLOCK, d),
                          eid.reshape(n_blk, TOK_BLOCK, PEER_HEADS, PEER_TOPK),
                          g.reshape(n_blk, TOK_BLOCK, PEER_HEADS, PEER_TOPK)))
    return out.reshape(b, s, d)


def setup_inputs(seed: int = 0) -> dict:
    key = jax.random.key(seed)
    ks = jax.random.split(key, 13)
    f32 = jnp.float32

    def nrm(k, shape, scale):
        return jax.random.normal(k, shape, f32) * scale

    x = nrm(ks[0], (BATCH, SEQ, D_MODEL), 1.0)
    ln1_g = 1.0 + nrm(ks[1], (DEPTH, D_MODEL), 0.02)
    w_in = nrm(ks[2], (DEPTH, D_MODEL, PROJ_WIDTH), D_MODEL ** -0.5)
    b_f = FORGET_BIAS_INIT + nrm(ks[3], (DEPTH, H_C), 0.1)
    rel_bias = nrm(ks[4], (DEPTH, H_A, 2 * REL_CLIP + 1), 0.5)
    w_out = nrm(ks[5], (DEPTH, MIX_WIDTH, D_MODEL), MIX_WIDTH ** -0.5)
    ln2_g = 1.0 + nrm(ks[6], (DEPTH, D_MODEL), 0.02)
    peer_wq = nrm(ks[7], (DEPTH, D_MODEL, PEER_HEADS * PEER_QDIM), D_MODEL ** -0.5)
    peer_k1 = nrm(ks[8], (DEPTH, PEER_HEADS, N_KEYS, PEER_HALF), PEER_HALF ** -0.5)
    peer_k2 = nrm(ks[9], (DEPTH, PEER_HEADS, N_KEYS, PEER_HALF), PEER_HALF ** -0.5)
    peer_u = nrm(ks[10], (DEPTH, N_EXPERTS, D_MODEL), D_MODEL ** -0.5)
    peer_v = nrm(ks[11], (DEPTH, N_EXPERTS, D_MODEL), PEER_HEADS ** -0.5)
    final_g = 1.0 + nrm(ks[12], (D_MODEL,), 0.02)
    return {'x': x, 'ln1_g': ln1_g, 'w_in': w_in, 'b_f': b_f, 'rel_bias': rel_bias,
            'w_out': w_out, 'ln2_g': ln2_g, 'peer_wq': peer_wq, 'peer_k1': peer_k1,
            'peer_k2': peer_k2, 'peer_u': peer_u, 'peer_v': peer_v, 'final_g': final_g}


def reference(x, ln1_g, w_in, b_f, rel_bias, w_out, ln2_g, peer_wq, peer_k1, peer_k2, peer_u, peer_v, final_g):
    for l in range(DEPTH):
        h = rms_norm(x, ln1_g[l])
        proj = jnp.einsum('bsd,de->bse', h, w_in[l])
        qa, qb, qc = _split_groups(proj[..., :MIX_WIDTH])
        ka, kb, kc = _split_groups(proj[..., MIX_WIDTH:2 * MIX_WIDTH])
        va, vb, vc = _split_groups(proj[..., 2 * MIX_WIDTH:3 * MIX_WIDTH])
        f_logit = (proj[..., 3 * MIX_WIDTH:] + b_f[l]).astype(jnp.float32)
        log_f = jax.nn.log_sigmoid(f_logit).transpose(0, 2, 1)
        o_a = chunked_relpos_attention(qa, ka, va, rel_bias[l])
        o_b = stick_breaking_attention(qb, kb, vb)
        o_c = forgetting_attention(qc, kc, vc, log_f)
        o = jnp.concatenate([_from_heads(o_a), _from_heads(o_b), _from_heads(o_c)], axis=-1)
        x = x + jnp.einsum('bse,ed->bsd', o.astype(x.dtype), w_out[l])
        x = x + peer_ffn(rms_norm(x, ln2_g[l]), peer_wq[l], peer_k1[l], peer_k2[l], peer_u[l], peer_v[l])
    return rms_norm(x, final_g)
```

```python
import functools
import math

import jax
import jax.numpy as jnp
from jax import lax
from jax.experimental import pallas as pl
from jax.experimental.pallas import tpu as pltpu

F32 = jnp.float32
BF16 = jnp.bfloat16

HEAD_DIM = 128
CHUNK = 64
N_LEFT_CHUNKS = 8
PEER_TOPK = 16
RMS_EPS = 1e-6
NEG_INF = -1e30

LANES = 128
V7X_VMEM_LIMIT_BYTES = 56 * 1024 * 1024

_NT_DIMS = (((1,), (1,)), ((), ()))


def _tile(dim, pref, align=LANES):
    if dim <= pref:
        return dim
    t = (pref // align) * align
    while t > align and dim % t:
        t -= align
    assert dim % t == 0, (dim, pref)
    return t


def _params(*semantics):
    return pltpu.CompilerParams(dimension_semantics=semantics, vmem_limit_bytes=V7X_VMEM_LIMIT_BYTES)


def _rmsnorm_kernel(x_ref, g_ref, o_ref):
    x = x_ref[...]
    ms = jnp.mean(x * x, axis=-1, keepdims=True)
    o_ref[...] = (x * lax.rsqrt(ms + RMS_EPS) * g_ref[...]).astype(o_ref.dtype)


def _rmsnorm(x, g, out_dtype):
    m, d = x.shape
    tm = _tile(m, 256, 8)
    return pl.pallas_call(
        _rmsnorm_kernel,
        grid=(m // tm,),
        in_specs=[pl.BlockSpec((tm, d), lambda i: (i, 0)), pl.BlockSpec((1, d), lambda i: (0, 0))],
        out_specs=pl.BlockSpec((tm, d), lambda i: (i, 0)),
        out_shape=jax.ShapeDtypeStruct((m, d), out_dtype),
        compiler_params=_params("parallel"),
        name="rmsnorm",
    )(x, g.reshape(1, d).astype(F32))


def _matmul_kernel(a_ref, b_ref, *rest, n_k, nt, has_res):
    if has_res:
        r_ref, o_ref, acc_ref = rest
    else:
        o_ref, acc_ref = rest
    k = pl.program_id(2)

    @pl.when(k == 0)
    def _():
        acc_ref[...] = jnp.zeros_like(acc_ref)

    if nt:
        acc_ref[...] += lax.dot_general(a_ref[...], b_ref[...], _NT_DIMS, preferred_element_type=F32)
    else:
        acc_ref[...] += jnp.dot(a_ref[...], b_ref[...], preferred_element_type=F32)

    @pl.when(k == n_k - 1)
    def _():
        acc = acc_ref[...]
        if has_res:
            acc = acc + r_ref[...]
        o_ref[...] = acc.astype(o_ref.dtype)


def _matmul(a, b, out_dtype, *, nt=False, residual=None, name="matmul"):
    m, kdim = a.shape
    n = b.shape[0] if nt else b.shape[1]
    tm, tn, tk = _tile(m, 1024), _tile(n, 1024), _tile(kdim, 2048)
    n_k = kdim // tk
    in_specs = [pl.BlockSpec((tm, tk), lambda i, j, k: (i, k))]
    if nt:
        in_specs.append(pl.BlockSpec((tn, tk), lambda i, j, k: (j, k)))
    else:
        in_specs.append(pl.BlockSpec((tk, tn), lambda i, j, k: (k, j)))
    args = [a, b]
    if residual is not None:
        in_specs.append(pl.BlockSpec((tm, tn), lambda i, j, k: (i, j)))
        args.append(residual)
    return pl.pallas_call(
        functools.partial(_matmul_kernel, n_k=n_k, nt=nt, has_res=residual is not None),
        grid=(m // tm, n // tn, n_k),
        in_specs=in_specs,
        out_specs=pl.BlockSpec((tm, tn), lambda i, j, k: (i, j)),
        out_shape=jax.ShapeDtypeStruct((m, n), out_dtype),
        scratch_shapes=[pltpu.VMEM((tm, tn), F32)],
        compiler_params=_params("parallel", "parallel", "arbitrary"),
        name=name,
    )(*args)


def _softplus(z):
    return jnp.maximum(z, 0.0) + jnp.log1p(jnp.exp(-jnp.abs(z)))


def _split3(x):
    hi = x.astype(BF16)
    r = x - hi.astype(F32)
    mid = r.astype(BF16)
    lo = (r - mid.astype(F32)).astype(BF16)
    return hi, mid, lo


def _forget_cumsum_kernel(fl_ref, b_ref, cum_ref, *, blk):
    s_len = fl_ref.shape[1]
    row = lax.broadcasted_iota(jnp.int32, (blk, blk), 0)
    col = lax.broadcasted_iota(jnp.int32, (blk, blk), 1)
    tri = (row >= col).astype(BF16)
    carry = jnp.zeros((1, fl_ref.shape[2]), F32)
    for i in range(s_len // blk):
        x = fl_ref[0, i * blk:(i + 1) * blk, :] + b_ref[...]
        log_f = -_softplus(-x)
        hi, mid, lo = _split3(log_f)
        c = (jnp.dot(tri, hi, preferred_element_type=F32) + jnp.dot(tri, mid, preferred_element_type=F32)
             + jnp.dot(tri, lo, preferred_element_type=F32)) + carry
        cum_ref[0, i * blk:(i + 1) * blk, :] = c
        carry = c[blk - 1:blk, :]


def _forget_cumsum(f_logit, b_f):
    b, s, w = f_logit.shape
    blk = _tile(s, 256)
    return pl.pallas_call(
        functools.partial(_forget_cumsum_kernel, blk=blk),
        grid=(b,),
        in_specs=[pl.BlockSpec((1, s, w), lambda i: (i, 0, 0)), pl.BlockSpec((1, w), lambda i: (0, 0))],
        out_specs=pl.BlockSpec((1, s, w), lambda i: (i, 0, 0)),
        out_shape=jax.ShapeDtypeStruct((b, s, w), F32),
        compiler_params=_params("parallel"),
        name="forget_cumsum",
    )(f_logit, b_f)


_A_TQ = 2 * CHUNK
_A_WIN = (N_LEFT_CHUNKS + 2) * CHUNK
_A_PAD = N_LEFT_CHUNKS * CHUNK


def _chunk_attn_kernel(q_ref, k_ref, v_ref, bias_ref, o_ref, kpad, vpad):
    s_len = q_ref.shape[1]
    scale = HEAD_DIM ** -0.5
    zeros = jnp.zeros((_A_PAD, HEAD_DIM), kpad.dtype)
    kpad[0:_A_PAD, :] = zeros
    vpad[0:_A_PAD, :] = zeros
    kpad[_A_PAD:_A_PAD + s_len, :] = k_ref[0]
    vpad[_A_PAD:_A_PAD + s_len, :] = v_ref[0]
    bias = bias_ref[0]
    col = lax.broadcasted_iota(jnp.int32, (_A_TQ, _A_WIN), 1)

    def body(qi, carry):
        q0 = pl.multiple_of(qi * _A_TQ, _A_TQ)
        q = q_ref[0, pl.ds(q0, _A_TQ), :]
        kw = kpad[pl.ds(q0, _A_WIN), :]
        vw = vpad[pl.ds(q0, _A_WIN), :]
        s = lax.dot_general(q, kw, _NT_DIMS, preferred_element_type=F32) * scale + bias
        s = jnp.where(col >= _A_PAD - q0, s, NEG_INF)
        m = jnp.max(s, axis=-1, keepdims=True)
        p = jnp.exp(s - m)
        l = jnp.sum(p, axis=-1, keepdims=True)
        o = jnp.dot(p.astype(BF16), vw, preferred_element_type=F32) / l
        o_ref[0, pl.ds(q0, _A_TQ), :] = o.astype(o_ref.dtype)
        return carry

    lax.fori_loop(0, s_len // _A_TQ, body, 0)


def _chunk_bias(rel_bias):
    clip = rel_bias.shape[-1] // 2
    i = jnp.arange(_A_TQ)[:, None]
    m = jnp.arange(_A_WIN)[None, :]
    rel = _A_PAD + i - m
    idx = jnp.clip(rel, -clip, clip) + clip
    qc = i // CHUNK
    kc = m // CHUNK
    valid = (kc >= qc) & (kc <= qc + N_LEFT_CHUNKS)
    return jnp.where(valid[None], rel_bias[:, idx].astype(F32), NEG_INF)


def _chunk_attention(proj, rel_bias, n_heads, col0, mix_blocks):
    b, s, _ = proj.shape
    bias = _chunk_bias(rel_bias)
    blk = (1, s, HEAD_DIM)
    return pl.pallas_call(
        _chunk_attn_kernel,
        grid=(b, n_heads),
        in_specs=[
            pl.BlockSpec(blk, lambda i, h: (i, 0, col0 + h)),
            pl.BlockSpec(blk, lambda i, h: (i, 0, mix_blocks + col0 + h)),
            pl.BlockSpec(blk, lambda i, h: (i, 0, 2 * mix_blocks + col0 + h)),
            pl.BlockSpec((1, _A_TQ, _A_WIN), lambda i, h: (h, 0, 0)),
        ],
        out_specs=pl.BlockSpec(blk, lambda i, h: (i, 0, h)),
        out_shape=jax.ShapeDtypeStruct((b, s, n_heads * HEAD_DIM), BF16),
        scratch_shapes=[pltpu.VMEM((s + _A_PAD, HEAD_DIM), BF16), pltpu.VMEM((s + _A_PAD, HEAD_DIM), BF16)],
        compiler_params=_params("parallel", "parallel"),
        name="chunk_attention",
    )(proj, proj, proj, bias)


def _stick_kernel(q_ref, k_ref, v_ref, o_ref, *, tq):
    s_len = q_ref.shape[1]
    scale = HEAD_DIM ** -0.5
    row = lax.broadcasted_iota(jnp.int32, (tq, tq), 0)
    col = lax.broadcasted_iota(jnp.int32, (tq, tq), 1)
    strict = col < row
    upper = (row > col).astype(BF16)

    def block(q, k0, masked, rsum, acc):
        k = k_ref[0, pl.ds(k0, tq), :]
        v = v_ref[0, pl.ds(k0, tq), :]
        z = lax.dot_general(q, k, _NT_DIMS, preferred_element_type=F32) * scale
        sp = _softplus(z)
        log_1m = -sp
        if masked:
            log_1m = jnp.where(strict, log_1m, 0.0)
        hi, mid, lo = _split3(log_1m)
        tail = (jnp.dot(hi, upper, preferred_element_type=F32) + jnp.dot(mid, upper, preferred_element_type=F32)
                + jnp.dot(lo, upper, preferred_element_type=F32)) + rsum
        a = jnp.exp(z - sp + tail)
        if masked:
            a = jnp.where(strict, a, 0.0)
        acc = acc + jnp.dot(a.astype(BF16), v, preferred_element_type=F32)
        rsum = rsum + jnp.sum(log_1m, axis=-1, keepdims=True)
        return rsum, acc

    def q_body(qi, carry):
        q0 = pl.multiple_of(qi * tq, tq)
        q = q_ref[0, pl.ds(q0, tq), :]
        rsum, acc = block(q, q0, True, jnp.zeros((tq, 1), F32), jnp.zeros((tq, HEAD_DIM), F32))

        def k_body(j, rc):
            k0 = pl.multiple_of((qi - 1 - j) * tq, tq)
            return block(q, k0, False, *rc)

        rsum, acc = lax.fori_loop(0, qi, k_body, (rsum, acc))
        o_ref[0, pl.ds(q0, tq), :] = acc.astype(o_ref.dtype)
        return carry

    lax.fori_loop(0, s_len // tq, q_body, 0)


def _stick_attention(proj, n_heads, col0, mix_blocks):
    b, s, _ = proj.shape
    tq = _tile(s, 256)
    blk = (1, s, HEAD_DIM)
    return pl.pallas_call(
        functools.partial(_stick_kernel, tq=tq),
        grid=(b, n_heads),
        in_specs=[
            pl.BlockSpec(blk, lambda i, h: (i, 0, col0 + h)),
            pl.BlockSpec(blk, lambda i, h: (i, 0, mix_blocks + col0 + h)),
            pl.BlockSpec(blk, lambda i, h: (i, 0, 2 * mix_blocks + col0 + h)),
        ],
        out_specs=pl.BlockSpec(blk, lambda i, h: (i, 0, h)),
        out_shape=jax.ShapeDtypeStruct((b, s, n_heads * HEAD_DIM), BF16),
        compiler_params=_params("parallel", "parallel"),
        name="stick_attention",
    )(proj, proj, proj)


def _fox_kernel(q_ref, k_ref, v_ref, cc_ref, cr_ref, o_ref, *, tq):
    s_len = q_ref.shape[1]
    scale = HEAD_DIM ** -0.5
    row = lax.broadcasted_iota(jnp.int32, (tq, tq), 0)
    col = lax.broadcasted_iota(jnp.int32, (tq, tq), 1)
    causal = col <= row

    def scores(q, cq, ki):
        k0 = pl.multiple_of(ki * tq, tq)
        k = k_ref[0, pl.ds(k0, tq), :]
        v = v_ref[0, pl.ds(k0, tq), :]
        ck = cr_ref[0, 0, pl.ds(ki, 1), :]
        s = lax.dot_general(q, k, _NT_DIMS, preferred_element_type=F32) * scale + cq - ck
        return s, v

    def q_body(qi, carry):
        q0 = pl.multiple_of(qi * tq, tq)
        q = q_ref[0, pl.ds(q0, tq), :]
        cq = cc_ref[0, 0, pl.ds(q0, tq), :]
        s, v = scores(q, cq, qi)
        s = jnp.where(causal, s, NEG_INF)
        m = jnp.max(s, axis=-1, keepdims=True)
        p = jnp.exp(s - m)
        l = jnp.sum(p, axis=-1, keepdims=True)
        acc = jnp.dot(p.astype(BF16), v, preferred_element_type=F32)

        def k_body(ki, mla):
            m, l, acc = mla
            s, v = scores(q, cq, ki)
            m_new = jnp.maximum(m, jnp.max(s, axis=-1, keepdims=True))
            alpha = jnp.exp(m - m_new)
            p = jnp.exp(s - m_new)
            l = alpha * l + jnp.sum(p, axis=-1, keepdims=True)
            acc = alpha * acc + jnp.dot(p.astype(BF16), v, preferred_element_type=F32)
            return m_new, l, acc

        m, l, acc = lax.fori_loop(0, qi, k_body, (m, l, acc))
        o_ref[0, pl.ds(q0, tq), :] = (acc / l).astype(o_ref.dtype)
        return carry

    lax.fori_loop(0, s_len // tq, q_body, 0)


def _fox_attention(proj, cum, n_heads, col0, mix_blocks):
    b, s, _ = proj.shape
    tq = _tile(s, 256)
    cum_h = jnp.transpose(cum[:, :, :n_heads], (0, 2, 1))
    cum_col = cum_h[..., None]
    cum_row = cum_h.reshape(b, n_heads, s // tq, tq)
    blk = (1, s, HEAD_DIM)
    return pl.pallas_call(
        functools.partial(_fox_kernel, tq=tq),
        grid=(b, n_heads),
        in_specs=[
            pl.BlockSpec(blk, lambda i, h: (i, 0, col0 + h)),
            pl.BlockSpec(blk, lambda i, h: (i, 0, mix_blocks + col0 + h)),
            pl.BlockSpec(blk, lambda i, h: (i, 0, 2 * mix_blocks + col0 + h)),
            pl.BlockSpec((1, 1, s, 1), lambda i, h: (i, h, 0, 0)),
            pl.BlockSpec((1, 1, s // tq, tq), lambda i, h: (i, h, 0, 0)),
        ],
        out_specs=pl.BlockSpec(blk, lambda i, h: (i, 0, h)),
        out_shape=jax.ShapeDtypeStruct((b, s, n_heads * HEAD_DIM), BF16),
        compiler_params=_params("parallel", "parallel"),
        name="fox_attention",
    )(proj, proj, proj, cum_col, cum_row)


def _extract_max(s, rows):
    m = jnp.max(s, axis=0, keepdims=True)
    first = jnp.min(jnp.where(s == m, rows, s.shape[0]), axis=0, keepdims=True)
    return m, jnp.where(rows == first, -jnp.inf, s)


def _top_desc(s, k):
    rows = lax.broadcasted_iota(jnp.int32, s.shape, 0)
    out = []
    for _ in range(k):
        m, s = _extract_max(s, rows)
        out.append(m)
    return jnp.concatenate(out, axis=0)


def _peer_score_kernel(qt_ref, k1_ref, k2_ref, s1_ref, c1_ref, s2_ref, e2_ref, tau_ref):
    n_keys = k1_ref.shape[1]
    half = k1_ref.shape[2]
    q1 = qt_ref[0:half, :]
    q2 = qt_ref[half:2 * half, :]
    s1 = jnp.dot(k1_ref[0], q1, preferred_element_type=F32)
    s2 = jnp.dot(k2_ref[0], q2, preferred_element_type=F32)
    v1 = _top_desc(s1, PEER_TOPK)
    v2 = _top_desc(s2, PEER_TOPK)
    cand = [v1[0:1, :] + v2] + [v1[a:a + 1, :] + v2[0:8, :] for a in range(1, PEER_TOPK)]
    cand = jnp.concatenate(cand, axis=0)
    rows = lax.broadcasted_iota(jnp.int32, cand.shape, 0)
    top = v1[0:1, :] + v2[0:1, :]
    z = jnp.zeros_like(top)
    m = top
    for _ in range(PEER_TOPK):
        m, cand = _extract_max(cand, rows)
        z = z + jnp.exp(m - top)
    s1_ref[0] = s1
    s2_ref[0] = s2
    c1_ref[0] = jnp.exp(s1 - v1[0:1, :]) / z
    e2_ref[0] = jnp.exp(s2 - v2[0:1, :])
    tau_ref[0] = m
    del n_keys


def _peer_scores(qt, k1, k2):
    n_heads, n_keys, half = k1.shape
    m = qt.shape[1]
    t = _tile(m, 512)
    big = pl.BlockSpec((1, n_keys, t), lambda i, h: (h, 0, i))
    kspec = pl.BlockSpec((1, n_keys, half), lambda i, h: (h, 0, 0))
    shape = jax.ShapeDtypeStruct((n_heads, n_keys, m), F32)
    return pl.pallas_call(
        _peer_score_kernel,
        grid=(m // t, n_heads),
        in_specs=[pl.BlockSpec((2 * half, t), lambda i, h: (h, i)), kspec, kspec],
        out_specs=[big, big, big, big, pl.BlockSpec((1, 1, t), lambda i, h: (h, 0, i))],
        out_shape=[shape, shape, shape, shape, jax.ShapeDtypeStruct((n_heads, 1, m), F32)],
        compiler_params=_params("parallel", "parallel"),
        name="peer_scores",
    )(qt, k1, k2)


def _peer_w_kernel(u_ref, h_ref, s1_ref, c1_ref, s2_ref, e2_ref, tau_ref, w_ref):
    n_heads, n_keys, _ = s2_ref.shape
    te = u_ref.shape[0]
    act_t = lax.dot_general(u_ref[...], h_ref[...], _NT_DIMS, preferred_element_type=F32)
    pieces = []
    for r in range(te // n_keys):
        a = act_t[r * n_keys:(r + 1) * n_keys, :]
        gelu = 0.5 * a * (1.0 + lax.erf(a * (1.0 / math.sqrt(2.0))))
        gate = jnp.zeros_like(a)
        for h in range(n_heads):
            s1 = s1_ref[h, r:r + 1, :]
            c1 = c1_ref[h, r:r + 1, :]
            selected = (s1 + s2_ref[h]) >= tau_ref[h]
            gate = gate + jnp.where(selected, e2_ref[h] * c1, 0.0)
        pieces.append(gelu * gate)
    w_t = jnp.concatenate(pieces, axis=0)
    w_ref[...] = w_t.T.astype(w_ref.dtype)


def _peer_weights(h2, u, s1, c1, s2, e2, tau):
    m, d = h2.shape
    e = u.shape[0]
    n_heads, n_keys, _ = s2.shape
    t = _tile(m, 512)
    te = 8 * n_keys
    assert e % te == 0
    small = pl.BlockSpec((n_heads, te // n_keys, t), lambda i, j: (0, j, i))
    big = pl.BlockSpec((n_heads, n_keys, t), lambda i, j: (0, 0, i))
    return pl.pallas_call(
        _peer_w_kernel,
        grid=(m // t, e // te),
        in_specs=[
            pl.BlockSpec((te, d), lambda i, j: (j, 0)),
            pl.BlockSpec((t, d), lambda i, j: (i, 0)),
            small, small, big, big,
            pl.BlockSpec((n_heads, 1, t), lambda i, j: (0, 0, i)),
        ],
        out_specs=pl.BlockSpec((t, te), lambda i, j: (i, j)),
        out_shape=jax.ShapeDtypeStruct((m, e), BF16),
        compiler_params=_params("parallel", "arbitrary"),
        name="peer_weights",
    )(u, h2, s1, c1, s2, e2, tau)


def kernel(x, ln1_g, w_in, b_f, rel_bias, w_out, ln2_g, peer_wq, peer_k1, peer_k2, peer_u, peer_v, final_g):
    b, s, d = x.shape
    m = b * s
    depth = w_in.shape[0]
    h_a = rel_bias.shape[1]
    h_c = b_f.shape[1]
    mix = w_out.shape[1]
    mix_blocks = mix // HEAD_DIM
    h_b = mix_blocks - h_a - h_c
    assert w_in.shape[2] == 3 * mix + h_c and h_c <= LANES

    xf = x.reshape(m, d)
    for l in range(depth):
        h = _rmsnorm(xf, ln1_g[l], BF16)
        w = w_in[l].astype(BF16)
        proj = _matmul(h, w[:, :3 * mix], BF16, name="in_proj").reshape(b, s, 3 * mix)
        w_f = jnp.pad(w[:, 3 * mix:], ((0, 0), (0, LANES - h_c)))
        f_logit = _matmul(h, w_f, F32, name="forget_proj").reshape(b, s, LANES)
        cum = _forget_cumsum(f_logit, jnp.pad(b_f[l].astype(F32), (0, LANES - h_c)).reshape(1, LANES))
        o_a = _chunk_attention(proj, rel_bias[l], h_a, 0, mix_blocks)
        o_b = _stick_attention(proj, h_b, h_a, mix_blocks)
        o_c = _fox_attention(proj, cum, h_c, h_a + h_b, mix_blocks)
        o = jnp.concatenate([o_a, o_b, o_c], axis=-1).reshape(m, mix)
        xf = _matmul(o, w_out[l].astype(BF16), F32, residual=xf, name="out_proj")

        h2 = _rmsnorm(xf, ln2_g[l], BF16)
        qt = _matmul(peer_wq[l].T.astype(BF16), h2, BF16, nt=True, name="peer_query")
        s1, c1, s2, e2, tau = _peer_scores(qt, peer_k1[l].astype(BF16), peer_k2[l].astype(BF16))
        w_peer = _peer_weights(h2, peer_u[l].astype(BF16), s1, c1, s2, e2, tau)
        xf = _matmul(w_peer, peer_v[l].astype(BF16), F32, residual=xf, name="peer_out")
    return _rmsnorm(xf, final_g, F32).reshape(b, s, d)
```
